```python
import jax
import jax.numpy as jnp
from jax import lax
import numpy as np

D_MODEL = 2048
BATCH = 16
SEQ = 256
DEPTH = 4
DEC_BATCH = 4
DEC_SEQ = 2048
PAST_LEN = 256

GRID_W = 64
N_MIXERS = 3
N_CONV_LAYERS = (DEPTH + 2) // 3
N_ATTN_LAYERS = (DEPTH + 1) // 3
N_DN_LAYERS = DEPTH // 3
N_MOD = 6
EPS = 1e-6
NEG_INF = -1e30

CONV_WIDTH = 31

HEAD_DIM = 128
N_HEADS = D_MODEL // HEAD_DIM
N_KV_HEADS = N_HEADS // 4
GROUP = N_HEADS // N_KV_HEADS
WINDOW = 128
BLOCK = 128
ROPE_BASE = 10000.0
QKV_DIM = (N_HEADS + 2 * N_KV_HEADS) * HEAD_DIM

DN_HEAD_DIM = 128
DN_QK_HEADS = D_MODEL // DN_HEAD_DIM
DN_V_HEADS = 2 * DN_QK_HEADS
DN_QK_DIM = DN_QK_HEADS * DN_HEAD_DIM
DN_V_DIM = DN_V_HEADS * DN_HEAD_DIM
DN_CONV_DIM = 2 * DN_QK_DIM + DN_V_DIM
DN_IN_DIM = DN_CONV_DIM + DN_V_DIM + 4 * DN_V_HEADS
DN_SHORT_CONV = 5
DN_CHUNK = 64

N_EXPERTS = 16
EC_FACTOR = 2
D_EXPERT = D_MODEL // 2

kernel_name = 'hybrid_dit_conv_swa_deltanet_ecmoe_step'


def rmsnorm(x, g):
    xf = x.astype(jnp.float32)
    y = xf * lax.rsqrt(jnp.mean(xf * xf, axis=-1, keepdims=True) + EPS)
    return (y * g.astype(jnp.float32)).astype(x.dtype)


def l2norm(x):
    xf = x.astype(jnp.float32)
    return xf * lax.rsqrt(jnp.sum(xf * xf, axis=-1, keepdims=True) + EPS)


def depthwise_conv(x, w):
    pad = w.shape[0] // 2
    return lax.conv_general_dilated(
        x, w.astype(x.dtype)[:, None, :], window_strides=(1,), padding=[(pad, pad)],
        dimension_numbers=('NWC', 'WIO', 'NWC'), feature_group_count=x.shape[-1])


def adaln(cvec, w_ada, b_ada):
    m = jax.nn.silu(cvec) @ w_ada + b_ada
    return m.reshape(cvec.shape[0], N_MOD, 1, D_MODEL)


def modulate(x, g, shift, scale):
    return rmsnorm(x, g) * (1.0 + scale) + shift


def conv_module(h, w_pw1, b_pw1, w_dw, b_dw, norm_g, w_pw2, b_pw2):
    a, gate = jnp.split(h @ w_pw1 + b_pw1, 2, axis=-1)
    u = a * jax.nn.sigmoid(gate)
    u = depthwise_conv(u, w_dw) + b_dw
    u = jax.nn.silu(rmsnorm(u, norm_g))
    return u @ w_pw2 + b_pw2


def split_qkv(h, w_qkv):
    B, T, _ = h.shape
    qkv = h @ w_qkv
    nq = N_HEADS * HEAD_DIM
    nk = N_KV_HEADS * HEAD_DIM
    q = qkv[..., :nq].reshape(B, T, N_HEADS, HEAD_DIM)
    k = qkv[..., nq:nq + nk].reshape(B, T, N_KV_HEADS, HEAD_DIM)
    v = qkv[..., nq + nk:].reshape(B, T, N_KV_HEADS, HEAD_DIM)
    return q, k, v


def sink_softmax(logits, sink):
    sk = jnp.broadcast_to(sink.astype(jnp.float32).reshape(N_KV_HEADS, GROUP, 1, 1),
                          logits.shape[:-1] + (1,))
    p = jax.nn.softmax(jnp.concatenate([logits, sk], axis=-1), axis=-1)
    return p[..., :-1]


def rope_tables(T):
    rows = T // GRID_W
    row = jnp.repeat(jnp.arange(rows, dtype=jnp.float32), GRID_W)
    col = (jnp.arange(rows * GRID_W) % GRID_W).astype(jnp.float32)
    n_freq = HEAD_DIM // 4
    inv = ROPE_BASE ** (-jnp.arange(n_freq, dtype=jnp.float32) / n_freq)
    ang = jnp.concatenate([row[:, None] * inv, col[:, None] * inv], axis=-1)
    return jnp.cos(ang), jnp.sin(ang)


def apply_rope(x, cos, sin):
    xf = x.astype(jnp.float32)
    x1, x2 = xf[..., :HEAD_DIM // 2], xf[..., HEAD_DIM // 2:]
    c = cos[None, :, None]
    s = sin[None, :, None]
    return jnp.concatenate([x1 * c - x2 * s, x1 * s + x2 * c], axis=-1).astype(x.dtype)


def attn_context(h, w_qkv, w_o, sink):
    B, S, _ = h.shape
    q, k, v = split_qkv(h, w_qkv)
    nb = S // BLOCK
    qb = jnp.moveaxis(q.reshape(B, nb, BLOCK, N_KV_HEADS, GROUP, HEAD_DIM), 1, 0)
    scale = HEAD_DIM ** -0.5

    def one_block(qi):
        s = jnp.einsum('bqhgd,bkhd->bhgqk', qi, k).astype(jnp.float32) * scale
        p = sink_softmax(s, sink).astype(v.dtype)
        return jnp.einsum('bhgqk,bkhd->bqhgd', p, v)

    o = lax.map(one_block, qb)
    o = jnp.moveaxis(o, 0, 1).reshape(B, S, N_HEADS * HEAD_DIM)
    return o @ w_o, k, v


def attn_latent(h, k_ctx, v_ctx, w_qkv, w_o, sink, cos, sin):
    B, T, _ = h.shape
    nb = T // BLOCK
    q, k, v = split_qkv(h, w_qkv)
    q = apply_rope(q, cos, sin)
    k = apply_rope(k, cos, sin)
    qb = q.reshape(B, nb, BLOCK, N_KV_HEADS, GROUP, HEAD_DIM)
    pad = ((0, 0), (BLOCK, BLOCK), (0, 0), (0, 0))
    kp = jnp.pad(k, pad).reshape(B, nb + 2, BLOCK, N_KV_HEADS, HEAD_DIM)
    vp = jnp.pad(v, pad).reshape(B, nb + 2, BLOCK, N_KV_HEADS, HEAD_DIM)
    k_band = jnp.concatenate([kp[:, :-2], kp[:, 1:-1], kp[:, 2:]], axis=2)
    v_band = jnp.concatenate([vp[:, :-2], vp[:, 1:-1], vp[:, 2:]], axis=2)
    blk = jnp.arange(nb)[:, None] * BLOCK
    q_pos = blk + jnp.arange(BLOCK)[None]
    k_pos = blk - BLOCK + jnp.arange(3 * BLOCK)[None]
    rel = k_pos[:, None, :] - q_pos[:, :, None]
    valid = (jnp.abs(rel) <= WINDOW) & (k_pos[:, None, :] >= 0) & (k_pos[:, None, :] < T)
    scale = HEAD_DIM ** -0.5
    s_loc = jnp.einsum('bnqhgd,bnkhd->bnhgqk', qb, k_band).astype(jnp.float32) * scale
    s_loc = jnp.where(valid[None, :, None, None], s_loc, NEG_INF)
    s_ctx = jnp.einsum('bnqhgd,bphd->bnhgqp', qb, k_ctx).astype(jnp.float32) * scale
    p = sink_softmax(jnp.concatenate([s_loc, s_ctx], axis=-1), sink).astype(v.dtype)
    n_loc = 3 * BLOCK
    o = (jnp.einsum('bnhgqk,bnkhd->bnqhgd', p[..., :n_loc], v_band)
         + jnp.einsum('bnhgqp,bphd->bnqhgd', p[..., n_loc:], v_ctx))
    return o.reshape(B, T, N_HEADS * HEAD_DIM) @ w_o


def dn_inputs(h, w_in, conv_w, a_log, dt_bias):
    B, T, _ = h.shape
    proj = h @ w_in
    qkv = jax.nn.silu(depthwise_conv(proj[..., :DN_CONV_DIM], conv_w))
    z = proj[..., DN_CONV_DIM:DN_CONV_DIM + DN_V_DIM].reshape(B, T, DN_V_HEADS, DN_HEAD_DIM)
    ab = proj[..., DN_CONV_DIM + DN_V_DIM:].astype(jnp.float32).reshape(B, T, 4, DN_V_HEADS)
    rep = DN_V_HEADS // DN_QK_HEADS
    q = l2norm(qkv[..., :DN_QK_DIM].reshape(B, T, DN_QK_HEADS, DN_HEAD_DIM)) * DN_HEAD_DIM ** -0.5
    k = l2norm(qkv[..., DN_QK_DIM:2 * DN_QK_DIM].reshape(B, T, DN_QK_HEADS, DN_HEAD_DIM))
    q = jnp.repeat(q, rep, axis=2)
    k = jnp.repeat(k, rep, axis=2)
    v = qkv[..., 2 * DN_QK_DIM:].reshape(B, T, DN_V_HEADS, DN_HEAD_DIM).astype(jnp.float32)
    g = -jnp.exp(a_log.astype(jnp.float32)) * jax.nn.softplus(ab[:, :, 0:2] + dt_bias.astype(jnp.float32))
    beta = jax.nn.sigmoid(ab[:, :, 2:4])
    return q, k, v, z, g, beta


def chunk_gated_delta(q, k, v, g, beta, s0):
    B, T, H, _ = q.shape
    DV = v.shape[-1]
    n = T // DN_CHUNK

    def blocks(x):
        return jnp.moveaxis(x.reshape((B, n, DN_CHUNK) + x.shape[2:]), 3, 2)

    q, k, v, g, beta = (blocks(x) for x in (q, k, v, g, beta))
    gc = jnp.cumsum(g, axis=-1)
    incl = jnp.tril(jnp.ones((DN_CHUNK, DN_CHUNK), dtype=bool))
    strict = jnp.tril(jnp.ones((DN_CHUNK, DN_CHUNK), dtype=bool), -1)
    decay = jnp.exp(jnp.where(incl, gc[..., :, None] - gc[..., None, :], NEG_INF))
    kb = k * beta[..., None]
    vb = v * beta[..., None]
    a_mat = jnp.where(strict, jnp.einsum('bnhcd,bnhsd->bnhcs', kb, k) * decay, 0.0)
    eye = jnp.eye(DN_CHUNK, dtype=jnp.float32)
    t_inv = lax.linalg.triangular_solve(eye + a_mat, jnp.broadcast_to(eye, a_mat.shape),
                                        left_side=True, lower=True)
    u = t_inv @ vb
    w = t_inv @ (kb * jnp.exp(gc)[..., None])
    qk = jnp.einsum('bnhcd,bnhsd->bnhcs', q, k) * decay
    qg = q * jnp.exp(gc)[..., None]
    kg = k * jnp.exp(gc[..., -1:] - gc)[..., None]
    glast = jnp.exp(gc[..., -1])
    xs = tuple(jnp.moveaxis(x, 1, 0) for x in (u, w, qk, qg, kg, glast))

    def step(S, inp):
        u_i, w_i, qk_i, qg_i, kg_i, gl_i = inp
        v_new = u_i - jnp.einsum('bhck,bhkv->bhcv', w_i, S)
        o_i = jnp.einsum('bhck,bhkv->bhcv', qg_i, S) + jnp.einsum('bhcs,bhsv->bhcv', qk_i, v_new)
        S = S * gl_i[..., None, None] + jnp.einsum('bhck,bhcv->bhkv', kg_i, v_new)
        return S, o_i

    s_final, o = lax.scan(step, s0.astype(jnp.float32), xs)
    o = jnp.moveaxis(jnp.moveaxis(o, 0, 1), 2, 3).reshape(B, T, H, DV)
    return o, s_final


def dn_bidir(q, k, v, g, beta, s_fwd, s_bwd):
    o_f, s_f = chunk_gated_delta(q, k, v, g[:, :, 0], beta[:, :, 0], s_fwd)
    fl = lambda x: jnp.flip(x, axis=1)
    o_b, s_b = chunk_gated_delta(fl(q), fl(k), fl(v), fl(g[:, :, 1]), fl(beta[:, :, 1]), s_bwd)
    return o_f + fl(o_b), s_f, s_b


def dn_output(o, z, norm_g, w_out):
    B, T = o.shape[:2]
    y = rmsnorm(o, norm_g) * jax.nn.silu(z.astype(jnp.float32))
    return y.reshape(B, T, DN_V_DIM).astype(z.dtype) @ w_out


def ec_moe(h, w_router, w_gate, w_up, w_down):
    B, N, _ = h.shape
    cap = EC_FACTOR * N // N_EXPERTS
    aff = jax.nn.softmax((h @ w_router).astype(jnp.float32), axis=-1)
    gate, idx = lax.top_k(jnp.swapaxes(aff, 1, 2), cap)
    bidx = jnp.arange(B)[:, None, None]
    xe = h[bidx, idx]
    hid = jax.nn.silu(jnp.einsum('becd,edf->becf', xe, w_gate)) * jnp.einsum('becd,edf->becf', xe, w_up)
    ye = jnp.einsum('becf,efd->becd', hid, w_down) * gate[..., None].astype(h.dtype)
    return jnp.zeros_like(h).at[bidx, idx].add(ye)


def setup_inputs(seed: int = 0) -> dict:
    key = jax.random.key(seed)
    ks = iter(jax.random.split(key, 48))

    def normal(shape, scale=1.0):
        return jax.random.normal(next(ks), shape, jnp.float32) * scale

    def gain(shape):
        return 1.0 + normal(shape, 0.05)

    D = D_MODEL
    inp = {}
    inp['x_prompt'] = normal((BATCH, SEQ, D))
    inp['x_sample'] = normal((DEC_BATCH, DEC_SEQ, D))
    inp['cache_attn_k'] = normal((DEC_BATCH, N_ATTN_LAYERS, PAST_LEN, N_KV_HEADS, HEAD_DIM))
    inp['cache_attn_v'] = normal((DEC_BATCH, N_ATTN_LAYERS, PAST_LEN, N_KV_HEADS, HEAD_DIM))
    st = (DEC_BATCH, N_DN_LAYERS, DN_V_HEADS, DN_HEAD_DIM, DN_HEAD_DIM)
    inp['state_dn_fwd'] = normal(st, DN_HEAD_DIM ** -0.5)
    inp['state_dn_bwd'] = normal(st, DN_HEAD_DIM ** -0.5)
    inp['c'] = normal((DEC_BATCH, D))
    inp['c_ctx'] = normal((D,))
    inp['w_ada'] = normal((DEPTH, D, N_MOD * D), 0.5 * D ** -0.5)
    inp['b_ada'] = normal((DEPTH, N_MOD * D), 0.02)
    inp['norm1_g'] = gain((DEPTH, D))
    inp['norm2_g'] = gain((DEPTH, D))
    inp['final_g'] = gain((D,))
    inp['conv_w_pw1'] = normal((N_CONV_LAYERS, D, 2 * D), D ** -0.5)
    inp['conv_b_pw1'] = normal((N_CONV_LAYERS, 2 * D), 0.02)
    inp['conv_w_dw'] = normal((N_CONV_LAYERS, CONV_WIDTH, D), CONV_WIDTH ** -0.5)
    inp['conv_b_dw'] = normal((N_CONV_LAYERS, D), 0.02)
    inp['conv_norm_g'] = gain((N_CONV_LAYERS, D))
    inp['conv_w_pw2'] = normal((N_CONV_LAYERS, D, D), D ** -0.5)
    inp['conv_b_pw2'] = normal((N_CONV_LAYERS, D), 0.02)
    inp['attn_w_qkv'] = normal((N_ATTN_LAYERS, D, QKV_DIM), D ** -0.5)
    inp['attn_w_o'] = normal((N_ATTN_LAYERS, N_HEADS * HEAD_DIM, D), (N_HEADS * HEAD_DIM) ** -0.5)
    inp['attn_sink'] = normal((N_ATTN_LAYERS, N_HEADS))
    inp['dn_w_in'] = normal((N_DN_LAYERS, D, DN_IN_DIM), D ** -0.5)
    inp['dn_conv_w'] = normal((N_DN_LAYERS, DN_SHORT_CONV, DN_CONV_DIM), DN_SHORT_CONV ** -0.5)
    inp['dn_A_log'] = jnp.log(jax.random.uniform(next(ks), (N_DN_LAYERS, 2, DN_V_HEADS), jnp.float32, 1.0, 16.0))
    dt = jnp.exp(jax.random.uniform(next(ks), (N_DN_LAYERS, 2, DN_V_HEADS), jnp.float32,
                                    float(np.log(1e-3)), float(np.log(1e-1))))
    inp['dn_dt_bias'] = dt + jnp.log(-jnp.expm1(-dt))
    inp['dn_norm_g'] = gain((N_DN_LAYERS, DN_HEAD_DIM))
    inp['dn_w_out'] = normal((N_DN_LAYERS, DN_V_DIM, D), DN_V_DIM ** -0.5)
    inp['moe_w_router'] = normal((DEPTH, D, N_EXPERTS), D ** -0.5)
    inp['moe_w_gate'] = normal((DEPTH, N_EXPERTS, D, D_EXPERT), D ** -0.5)
    inp['moe_w_up'] = normal((DEPTH, N_EXPERTS, D, D_EXPERT), D ** -0.5)
    inp['moe_w_down'] = normal((DEPTH, N_EXPERTS, D_EXPERT, D), D_EXPERT ** -0.5)
    return inp


def reference(x_prompt, x_sample, cache_attn_k, cache_attn_v, state_dn_fwd, state_dn_bwd,
              c, c_ctx, w_ada, b_ada, norm1_g, norm2_g, final_g,
              conv_w_pw1, conv_b_pw1, conv_w_dw, conv_b_dw, conv_norm_g, conv_w_pw2, conv_b_pw2,
              attn_w_qkv, attn_w_o, attn_sink,
              dn_w_in, dn_conv_w, dn_A_log, dn_dt_bias, dn_norm_g, dn_w_out,
              moe_w_router, moe_w_gate, moe_w_up, moe_w_down):
    cos, sin = rope_tables(x_sample.shape[1])
    xp, xs = x_prompt, x_sample
    new_k, new_v, new_sf, new_sb = [], [], [], []
    for l in range(DEPTH):
        j = l // N_MIXERS
        kind = l % N_MIXERS
        mp = adaln(c_ctx[None], w_ada[l], b_ada[l])
        ms = adaln(c, w_ada[l], b_ada[l])
        hp = modulate(xp, norm1_g[l], mp[:, 0], mp[:, 1])
        hs = modulate(xs, norm1_g[l], ms[:, 0], ms[:, 1])
        if kind == 0:
            cargs = (conv_w_pw1[j], conv_b_pw1[j], conv_w_dw[j], conv_b_dw[j],
                     conv_norm_g[j], conv_w_pw2[j], conv_b_pw2[j])
            yp = conv_module(hp, *cargs)
            ys = conv_module(hs, *cargs)
        elif kind == 1:
            yp, k_ctx, v_ctx = attn_context(hp, attn_w_qkv[j], attn_w_o[j], attn_sink[j])
            ys = attn_latent(hs, cache_attn_k[:, j], cache_attn_v[:, j], attn_w_qkv[j],
                             attn_w_o[j], attn_sink[j], cos, sin)
            new_k.append(k_ctx)
            new_v.append(v_ctx)
        else:
            q, k, v, z, g, beta = dn_inputs(hp, dn_w_in[j], dn_conv_w[j], dn_A_log[j], dn_dt_bias[j])
            s_zero = jnp.zeros((hp.shape[0], DN_V_HEADS, DN_HEAD_DIM, DN_HEAD_DIM), jnp.float32)
            o, s_f, s_b = dn_bidir(q, k, v, g, beta, s_zero, s_zero)
            yp = dn_output(o, z, dn_norm_g[j], dn_w_out[j])
            q, k, v, z, g, beta = dn_inputs(hs, dn_w_in[j], dn_conv_w[j], dn_A_log[j], dn_dt_bias[j])
            o, _, _ = dn_bidir(q, k, v, g, beta, state_dn_fwd[:, j], state_dn_bwd[:, j])
            ys = dn_output(o, z, dn_norm_g[j], dn_w_out[j])
            new_sf.append(s_f.astype(x_prompt.dtype))
            new_sb.append(s_b.astype(x_prompt.dtype))
        xp = xp + mp[:, 2] * yp
        xs = xs + ms[:, 2] * ys
        hp = modulate(xp, norm2_g[l], mp[:, 3], mp[:, 4])
        hs = modulate(xs, norm2_g[l], ms[:, 3], ms[:, 4])
        margs = (moe_w_router[l], moe_w_gate[l], moe_w_up[l], moe_w_down[l])
        xp = xp + mp[:, 5] * ec_moe(hp, *margs)
        xs = xs + ms[:, 5] * ec_moe(hs, *margs)
    y_prompt = rmsnorm(xp, final_g)
    y_sample = rmsnorm(xs, final_g)
    new_cache_attn_k = jnp.stack(new_k, axis=1)
    new_cache_attn_v = jnp.stack(new_v, axis=1)
    new_state_dn_fwd = jnp.stack(new_sf, axis=1)
    new_state_dn_bwd = jnp.stack(new_sb, axis=1)
    return (y_prompt, y_sample, new_cache_attn_k, new_cache_attn_v, new_state_dn_fwd, new_state_dn_bwd)
```

```python
import functools

import jax
import jax.numpy as jnp
from jax import lax
from jax.experimental import pallas as pl
from jax.experimental.pallas import tpu as pltpu

F32 = jnp.float32
BF16 = jnp.bfloat16
HIGHEST = lax.Precision.HIGHEST

EPS = 1e-6
NEG_INF = -1e30
N_MOD = 6
MOD_ROWS = 8
ROW_TILE = 256
V7X_VMEM_LIMIT = 56 * 1024 * 1024

CONV_WIDTH = 31
HEAD_DIM = 128
N_HEADS = 16
N_KV_HEADS = 4
GROUP = 4
WINDOW = 128
ATT_BLOCK = 128
ROPE_BASE = 10000.0
GRID_W = 64
DN_HEAD_DIM = 128
DN_QK_HEADS = 16
DN_V_HEADS = 32
DN_SHORT_CONV = 5
DN_CHUNK = 64
N_EXPERTS = 16
EC_FACTOR = 2


def _params(n_axes):
    return pltpu.CompilerParams(dimension_semantics=("arbitrary",) * n_axes,
                                vmem_limit_bytes=V7X_VMEM_LIMIT)


class Layout:
    def __init__(self, n_prompt, prompt_len, n_sample, sample_len):
        self.n_prompt, self.prompt_len = n_prompt, prompt_len
        self.n_sample, self.sample_len = n_sample, sample_len
        self.prompt_rows = n_prompt * prompt_len
        self.sample_rows = n_sample * sample_len
        self.rows = self.prompt_rows + self.sample_rows

    def group(self, i, tile):
        npt = self.prompt_rows // tile
        tps = self.sample_len // tile
        return jnp.where(i < npt, 0, 1 + (i - npt) // tps)


def _mod_spec(lay, layer, slot, tile, row_axis, width):
    def index(*ids):
        return ((layer * MOD_ROWS + lay.group(ids[row_axis], tile)) * N_MOD + slot, 0, 0)

    return pl.BlockSpec((None, 1, width), index)


def _adaln_kernel(c_ref, w_ref, b_ref, o_ref):
    c = c_ref[...]
    s = (c * jax.nn.sigmoid(c)).astype(BF16)
    acc = jnp.dot(s, w_ref[...].astype(BF16), preferred_element_type=F32)
    o_ref[...] = acc + b_ref[...]


def adaln_all(cvec, w_ada, b_ada, tn=1024):
    L, D, N = w_ada.shape
    return pl.pallas_call(
        _adaln_kernel,
        grid=(L, N // tn),
        in_specs=[pl.BlockSpec((MOD_ROWS, D), lambda l, j: (0, 0)),
                  pl.BlockSpec((None, D, tn), lambda l, j: (l, 0, j)),
                  pl.BlockSpec((None, 1, tn), lambda l, j: (l, 0, j))],
        out_specs=pl.BlockSpec((None, MOD_ROWS, tn), lambda l, j: (l, 0, j)),
        out_shape=jax.ShapeDtypeStruct((L, MOD_ROWS, N), F32),
        compiler_params=_params(2),
    )(cvec, w_ada, b_ada.reshape(L, 1, N))


def _modulate_kernel(x_ref, g_ref, sh_ref, sc_ref, *rest, router):
    x = x_ref[...]
    y = x * lax.rsqrt(jnp.mean(x * x, axis=-1, keepdims=True) + EPS) * g_ref[...]
    h = y * (1.0 + sc_ref[...]) + sh_ref[...]
    if router:
        wr_ref, h_ref, aff_ref = rest
        logits = jnp.dot(h, wr_ref[...], precision=HIGHEST, preferred_element_type=F32)
        m = jnp.max(logits, axis=-1, keepdims=True)
        e = jnp.exp(logits - m)
        aff_ref[...] = e / jnp.sum(e, axis=-1, keepdims=True)
    else:
        (h_ref,) = rest
    h_ref[...] = h.astype(BF16)


def modulate(lay, x, g, mod3, layer, shift_slot, scale_slot, w_router=None):
    T, D = x.shape
    router = w_router is not None
    tile = ROW_TILE
    in_specs = [pl.BlockSpec((tile, D), lambda i: (i, 0)),
                pl.BlockSpec((1, D), lambda i: (0, 0)),
                _mod_spec(lay, layer, shift_slot, tile, 0, D),
                _mod_spec(lay, layer, scale_slot, tile, 0, D)]
    args = [x, g.reshape(1, D), mod3, mod3]
    out_specs = [pl.BlockSpec((tile, D), lambda i: (i, 0))]
    out_shape = [jax.ShapeDtypeStruct((T, D), BF16)]
    if router:
        E = w_router.shape[1]
        in_specs.append(pl.BlockSpec((D, E), lambda i: (0, 0)))
        args.append(w_router)
        out_specs.append(pl.BlockSpec((tile, E), lambda i: (i, 0)))
        out_shape.append(jax.ShapeDtypeStruct((T, E), F32))
    return pl.pallas_call(
        functools.partial(_modulate_kernel, router=router),
        grid=(T // tile,), in_specs=in_specs, out_specs=out_specs, out_shape=out_shape,
        compiler_params=_params(1),
    )(*args)


def _mm_kernel(*refs, n_w, has_bias, mode):
    it = iter(refs)
    x_ref = next(it)
    w_refs = [next(it) for _ in range(n_w)]
    b_refs = [next(it) for _ in range(n_w)] if has_bias else []
    if mode == "res":
        xres_ref, gate_ref = next(it), next(it)
    o_ref = next(it)
    wbf_ref = next(it)

    @pl.when(pl.program_id(1) == 0)
    def _():
        for k in range(n_w):
            wbf_ref[k] = w_refs[k][...].astype(BF16)

    x = x_ref[...]
    accs = []
    for k in range(n_w):
        a = jnp.dot(x, wbf_ref[k], preferred_element_type=F32)
        if has_bias:
            a = a + b_refs[k][...]
        accs.append(a)
    if mode == "glu":
        out = accs[0] * jax.nn.sigmoid(accs[1])
    elif mode == "res":
        out = xres_ref[...] + gate_ref[...] * accs[0]
    else:
        out = accs[0]
    o_ref[...] = out.astype(o_ref.dtype)


def token_matmul(x, w, *, col_blocks, tn, n_out_cols, bias=None, mode="plain", residual=None,
                 out_dtype=F32, tm=512):
    T, K = x.shape
    n_w = len(col_blocks)
    has_bias = bias is not None
    in_specs = [pl.BlockSpec((tm, K), lambda j, i: (i, 0))]
    args = [x]
    for cb in col_blocks:
        in_specs.append(pl.BlockSpec((K, tn), lambda j, i, cb=cb: (0, cb + j)))
        args.append(w)
    if has_bias:
        b2 = bias.reshape(1, -1)
        for cb in col_blocks:
            in_specs.append(pl.BlockSpec((1, tn), lambda j, i, cb=cb: (0, cb + j)))
            args.append(b2)
    if mode == "res":
        lay, x_res, mod3, layer, gate_slot = residual
        in_specs.append(pl.BlockSpec((tm, tn), lambda j, i: (i, j)))
        args.append(x_res)
        in_specs.append(_mod_spec_tn(lay, layer, gate_slot, tm, tn))
        args.append(mod3)
    return pl.pallas_call(
        functools.partial(_mm_kernel, n_w=n_w, has_bias=has_bias, mode=mode),
        grid=(n_out_cols // tn, T // tm),
        in_specs=in_specs,
        out_specs=pl.BlockSpec((tm, tn), lambda j, i: (i, j)),
        out_shape=jax.ShapeDtypeStruct((T, n_out_cols), out_dtype),
        scratch_shapes=[pltpu.VMEM((n_w, K, tn), BF16)],
        compiler_params=_params(2),
    )(*args)


def _mod_spec_tn(lay, layer, slot, tm, tn):
    def index(j, i):
        return ((layer * MOD_ROWS + lay.group(i, tm)) * N_MOD + slot, 0, j)

    return pl.BlockSpec((None, 1, tn), index)


def _dwconv_kernel(prev_ref, cur_ref, next_ref, w_ref, b_ref, g_ref, o_ref, buf_ref, conv_ref, *,
                   lay, halo, width):
    i = pl.program_id(0)
    tile, D = cur_ref.shape
    npt = lay.prompt_rows // tile
    tps = lay.sample_len // tile
    ppt = lay.prompt_len // tile
    pos = jnp.where(i < npt, i % ppt, (i - npt) % tps)
    last = jnp.where(i < npt, ppt - 1, tps - 1)
    has_prev = (pos > 0).astype(F32)
    has_next = (pos < last).astype(F32)
    buf_ref[0:halo, :] = prev_ref[...] * has_prev
    buf_ref[halo:halo + tile, :] = cur_ref[...]
    buf_ref[halo + tile:, :] = next_ref[...] * has_next
    pad = width // 2
    lanes = 128
    rows = 64
    for r in range(tile // rows):
        for c in range(D // lanes):
            cs = slice(c * lanes, (c + 1) * lanes)
            acc = jnp.zeros((rows, lanes), F32)
            for k in range(width):
                start = r * rows + halo - pad + k
                acc = acc + buf_ref[start:start + rows, cs] * w_ref[k:k + 1, cs]
            conv_ref[r * rows:(r + 1) * rows, cs] = acc + b_ref[:, cs]
    u = conv_ref[...]
    y = u * lax.rsqrt(jnp.mean(u * u, axis=-1, keepdims=True) + EPS) * g_ref[...]
    o_ref[...] = (y * jax.nn.sigmoid(y)).astype(o_ref.dtype)


def dwconv_norm_swish(lay, u, w_dw, b_dw, norm_g):
    T, D = u.shape
    tile, halo = ROW_TILE, 16
    width = w_dw.shape[0]
    hb = tile // halo
    n_halo_blocks = T // halo
    return pl.pallas_call(
        functools.partial(_dwconv_kernel, lay=lay, halo=halo, width=width),
        grid=(T // tile,),
        in_specs=[pl.BlockSpec((halo, D), lambda i: (jnp.maximum(i * hb - 1, 0), 0)),
                  pl.BlockSpec((tile, D), lambda i: (i, 0)),
                  pl.BlockSpec((halo, D), lambda i: (jnp.minimum((i + 1) * hb, n_halo_blocks - 1), 0)),
                  pl.BlockSpec((width, D), lambda i: (0, 0)),
                  pl.BlockSpec((1, D), lambda i: (0, 0)),
                  pl.BlockSpec((1, D), lambda i: (0, 0))],
        out_specs=pl.BlockSpec((tile, D), lambda i: (i, 0)),
        out_shape=jax.ShapeDtypeStruct((T, D), BF16),
        scratch_shapes=[pltpu.VMEM((tile + 2 * halo, D), F32), pltpu.VMEM((tile, D), F32)],
        compiler_params=_params(1),
    )(u, u, u, w_dw, b_dw.reshape(1, D), norm_g.reshape(1, D))


def conv_layer(lay, x, h, mod3, layer, w_pw1, b_pw1, w_dw, b_dw, norm_g, w_pw2, b_pw2):
    D = x.shape[1]
    tn = 512
    u = token_matmul(h, w_pw1, col_blocks=(0, D // tn), tn=tn, n_out_cols=D, bias=b_pw1, mode="glu")
    uc = dwconv_norm_swish(lay, u, w_dw, b_dw, norm_g)
    return token_matmul(uc, w_pw2, col_blocks=(0,), tn=tn, n_out_cols=D, bias=b_pw2, mode="res",
                        residual=(lay, x, mod3, layer, 2))


def _rank_kernel(aff_ref, afft_ref, rank_ref, *, jt):
    N, E = aff_ref.shape
    lane_idx = lax.broadcasted_iota(jnp.int32, (1, N), 1)
    for e in range(E):
        arow = afft_ref[e:e + 1, :]
        acc = jnp.zeros((1, N), jnp.int32)
        for t in range(N // jt):
            acol = aff_ref[t * jt:(t + 1) * jt, e:e + 1]
            row_idx = lax.broadcasted_iota(jnp.int32, (jt, 1), 0) + t * jt
            before = (acol > arow) | ((acol == arow) & (row_idx < lane_idx))
            acc = acc + jnp.sum(before.astype(jnp.int32), axis=0, keepdims=True)
        rank_ref[e:e + 1, :] = acc


def expert_ranks(aff, aff_t, n_seq, seq_len, row0):
    E = aff.shape[1]
    b0 = row0 // seq_len
    return pl.pallas_call(
        functools.partial(_rank_kernel, jt=min(seq_len, 256)),
        grid=(n_seq,),
        in_specs=[pl.BlockSpec((seq_len, E), lambda b: (b0 + b, 0)),
                  pl.BlockSpec((E, seq_len), lambda b: (0, b0 + b))],
        out_specs=pl.BlockSpec((None, E, seq_len), lambda b: (b, 0, 0)),
        out_shape=jax.ShapeDtypeStruct((n_seq, E, seq_len), jnp.int32),
        compiler_params=_params(1),
    )(aff, aff_t)


def _gather_kernel(*refs, cap, aliased):
    if aliased:
        _, _, rank_ref, afft_ref, h_ref, xe_ref, gate_ref = refs
    else:
        rank_ref, afft_ref, h_ref, xe_ref, gate_ref = refs
    N = h_ref.shape[0]
    slot = lax.broadcasted_iota(jnp.int32, (cap, 1), 0)
    hit = slot == rank_ref[...]
    xe_ref[...] = jnp.dot(hit.astype(BF16), h_ref[...], preferred_element_type=F32).astype(xe_ref.dtype)
    gate_ref[...] = jnp.sum(jnp.where(hit, afft_ref[...], 0.0), axis=-1, keepdims=True)


def expert_gather(h, rank, aff_t3, n_seq, seq_len, row0, slot0, total_slots, prev=None):
    T, D = h.shape
    E = rank.shape[1]
    cap = EC_FACTOR * seq_len // E
    b0 = row0 // seq_len
    s0 = slot0 // cap
    rank3 = rank.reshape(n_seq * E, 1, seq_len)
    in_specs = [pl.BlockSpec((None, 1, seq_len), lambda b, e: (b * E + e, 0, 0)),
                pl.BlockSpec((None, 1, seq_len), lambda b, e: (e, 0, b0 + b)),
                pl.BlockSpec((seq_len, D), lambda b, e: (b0 + b, 0))]
    args = [rank3, aff_t3, h]
    aliases = {}
    if prev is not None:
        in_specs = [pl.BlockSpec(memory_space=pl.ANY), pl.BlockSpec(memory_space=pl.ANY)] + in_specs
        args = list(prev) + args
        aliases = {0: 0, 1: 1}
    return pl.pallas_call(
        functools.partial(_gather_kernel, cap=cap, aliased=prev is not None),
        grid=(n_seq, E),
        in_specs=in_specs,
        out_specs=[pl.BlockSpec((None, cap, D), lambda b, e: (e, s0 + b, 0)),
                   pl.BlockSpec((None, cap, 1), lambda b, e: (e, s0 + b, 0))],
        out_shape=[jax.ShapeDtypeStruct((E, total_slots, D), BF16),
                   jax.ShapeDtypeStruct((E, total_slots, 1), F32)],
        input_output_aliases=aliases,
        compiler_params=_params(2),
    )(*args)


def _ffn_up_kernel(x_ref, wg_ref, wu_ref, o_ref):
    x = x_ref[...]
    a = jnp.dot(x, wg_ref[...].astype(BF16), preferred_element_type=F32)
    b = jnp.dot(x, wu_ref[...].astype(BF16), preferred_element_type=F32)
    o_ref[...] = (a * jax.nn.sigmoid(a) * b).astype(o_ref.dtype)


def _ffn_down_kernel(x_ref, w_ref, gate_ref, o_ref):
    y = jnp.dot(x_ref[...], w_ref[...].astype(BF16), preferred_element_type=F32)
    o_ref[...] = (y * gate_ref[...]).astype(o_ref.dtype)


def expert_ffn(xe, gate, w_gate, w_up, w_down, tf=512, tn=1024):
    E, R, D = xe.shape
    F = w_gate.shape[2]
    hid = pl.pallas_call(
        _ffn_up_kernel,
        grid=(E, F // tf),
        in_specs=[pl.BlockSpec((None, R, D), lambda e, j: (e, 0, 0)),
                  pl.BlockSpec((None, D, tf), lambda e, j: (e, 0, j)),
                  pl.BlockSpec((None, D, tf), lambda e, j: (e, 0, j))],
        out_specs=pl.BlockSpec((None, R, tf), lambda e, j: (e, 0, j)),
        out_shape=jax.ShapeDtypeStruct((E, R, F), BF16),
        compiler_params=_params(2),
    )(xe, w_gate, w_up)
    return pl.pallas_call(
        _ffn_down_kernel,
        grid=(E, D // tn),
        in_specs=[pl.BlockSpec((None, R, F), lambda e, j: (e, 0, 0)),
                  pl.BlockSpec((None, F, tn), lambda e, j: (e, 0, j)),
                  pl.BlockSpec((None, R, 1), lambda e, j: (e, 0, 0))],
        out_specs=pl.BlockSpec((None, R, tn), lambda e, j: (e, 0, j)),
        out_shape=jax.ShapeDtypeStruct((E, R, D), BF16),
        compiler_params=_params(2),
    )(hid, w_down, gate)


def _scatter_kernel(*refs, cap, aliased):
    if aliased:
        _, rankc_ref, ye_ref, x_ref, gate_ref, o_ref = refs
    else:
        rankc_ref, ye_ref, x_ref, gate_ref, o_ref = refs
    E = ye_ref.shape[0]
    slot = lax.broadcasted_iota(jnp.int32, (1, cap), 1)
    acc = jnp.zeros(x_ref.shape, F32)
    for e in range(E):
        hit = rankc_ref[:, e:e + 1] == slot
        acc = acc + jnp.dot(hit.astype(BF16), ye_ref[e], preferred_element_type=F32)
    o_ref[...] = x_ref[...] + gate_ref[...] * acc


def expert_scatter(lay, x, ye, rank_col, mod3, layer, n_seq, seq_len, row0, slot0, prev=None, tn=512):
    T, D = x.shape
    E = ye.shape[0]
    cap = EC_FACTOR * seq_len // E
    b0 = row0 // seq_len
    s0 = slot0 // cap
    grp0 = 0 if row0 == 0 else 1

    def gate_index(b, j):
        g = 0 if row0 == 0 else 1 + b
        return ((layer * MOD_ROWS + g) * N_MOD + 5, 0, j)

    in_specs = [pl.BlockSpec((None, seq_len, E), lambda b, j: (b, 0, 0)),
                pl.BlockSpec((E, cap, tn), lambda b, j: (0, s0 + b, j)),
                pl.BlockSpec((seq_len, tn), lambda b, j: (b0 + b, j)),
                pl.BlockSpec((None, 1, tn), gate_index)]
    args = [rank_col, ye, x, mod3]
    aliases = {}
    if prev is not None:
        in_specs = [pl.BlockSpec(memory_space=pl.ANY)] + in_specs
        args = [prev] + args
        aliases = {0: 0}
    return pl.pallas_call(
        functools.partial(_scatter_kernel, cap=cap, aliased=prev is not None),
        grid=(n_seq, D // tn),
        in_specs=in_specs,
        out_specs=pl.BlockSpec((seq_len, tn), lambda b, j: (b0 + b, j)),
        out_shape=jax.ShapeDtypeStruct((T, D), F32),
        input_output_aliases=aliases,
        compiler_params=_params(2),
    )(*args)


def moe_layer(lay, x, h, aff, mod3, layer, w_gate, w_up, w_down):
    E = aff.shape[1]
    aff_t = aff.T
    aff_t3 = aff_t.reshape(E, 1, -1)
    cap_p = EC_FACTOR * lay.prompt_len // E
    cap_s = EC_FACTOR * lay.sample_len // E
    slots_p = lay.n_prompt * cap_p
    total = slots_p + lay.n_sample * cap_s
    rank_p = expert_ranks(aff, aff_t, lay.n_prompt, lay.prompt_len, 0)
    rank_s = expert_ranks(aff, aff_t, lay.n_sample, lay.sample_len, lay.prompt_rows)
    part = expert_gather(h, rank_p, aff_t3, lay.n_prompt, lay.prompt_len, 0, 0, total)
    xe, gate = expert_gather(h, rank_s, aff_t3, lay.n_sample, lay.sample_len, lay.prompt_rows, slots_p, total,
                             prev=part)
    ye = expert_ffn(xe, gate, w_gate, w_up, w_down)
    xp = expert_scatter(lay, x, ye, jnp.swapaxes(rank_p, 1, 2), mod3, layer, lay.n_prompt, lay.prompt_len, 0, 0)
    return expert_scatter(lay, x, ye, jnp.swapaxes(rank_s, 1, 2), mod3, layer, lay.n_sample, lay.sample_len,
                          lay.prompt_rows, slots_p, prev=xp)


def _rope_kernel(x_ref, cos_ref, sin_ref, o_ref, *, n_rot):
    cosf, sinf = cos_ref[...], sin_ref[...]
    n_heads = x_ref.shape[1] // HEAD_DIM
    for hd in range(n_heads):
        cs = slice(hd * HEAD_DIM, (hd + 1) * HEAD_DIM)
        x = x_ref[:, cs]
        if hd < n_rot:
            x = x * cosf + pltpu.roll(x, HEAD_DIM // 2, axis=1) * sinf
        o_ref[:, cs] = x.astype(o_ref.dtype)


def rope_sample(lay, qkv, cosf, sinf):
    W = qkv.shape[1]
    tile = ROW_TILE
    npt = lay.prompt_rows // tile
    tps = lay.sample_len // tile
    return pl.pallas_call(
        functools.partial(_rope_kernel, n_rot=N_HEADS + N_KV_HEADS),
        grid=(lay.sample_rows // tile,),
        in_specs=[pl.BlockSpec((tile, W), lambda i: (npt + i, 0)),
                  pl.BlockSpec((tile, HEAD_DIM), lambda i: (i % tps, 0)),
                  pl.BlockSpec((tile, HEAD_DIM), lambda i: (i % tps, 0))],
        out_specs=pl.BlockSpec((tile, W), lambda i: (i, 0)),
        out_shape=jax.ShapeDtypeStruct((lay.sample_rows, W), BF16),
        compiler_params=_params(1),
    )(qkv, cosf, sinf)


def _stack_heads(q_ref):
    return jnp.concatenate([q_ref[:, g * HEAD_DIM:(g + 1) * HEAD_DIM].astype(BF16) for g in range(GROUP)], axis=0)


def _sink_column(sink_ref, hk, rows):
    return jnp.concatenate([jnp.full((rows, 1), sink_ref[hk * GROUP + g], F32) for g in range(GROUP)], axis=0)


def _dot_nt(a, b):
    return lax.dot_general(a, b, (((1,), (1,)), ((), ())), preferred_element_type=F32)


def _ctx_attn_kernel(sink_ref, q_ref, k_ref, v_ref, o_ref):
    S = q_ref.shape[0]
    hk = pl.program_id(1)
    q = _stack_heads(q_ref)
    s = _dot_nt(q, k_ref[...].astype(BF16)) * (HEAD_DIM ** -0.5)
    sink = _sink_column(sink_ref, hk, S)
    m = jnp.maximum(jnp.max(s, axis=-1, keepdims=True), sink)
    e = jnp.exp(s - m)
    den = jnp.sum(e, axis=-1, keepdims=True) + jnp.exp(sink - m)
    o = jnp.dot((e / den).astype(BF16), v_ref[...].astype(BF16), preferred_element_type=F32)
    for g in range(GROUP):
        o_ref[:, g * HEAD_DIM:(g + 1) * HEAD_DIM] = o[g * S:(g + 1) * S].astype(o_ref.dtype)


def attn_context(lay, qkv, sink, total_rows):
    S = lay.prompt_len
    qw = GROUP * HEAD_DIM
    kb0 = N_HEADS
    vb0 = N_HEADS + N_KV_HEADS
    return pl.pallas_call(
        _ctx_attn_kernel,
        grid=(lay.n_prompt, N_KV_HEADS),
        in_specs=[pl.BlockSpec(memory_space=pltpu.SMEM),
                  pl.BlockSpec((S, qw), lambda b, h: (b, h)),
                  pl.BlockSpec((S, HEAD_DIM), lambda b, h: (b, kb0 + h)),
                  pl.BlockSpec((S, HEAD_DIM), lambda b, h: (b, vb0 + h))],
        out_specs=pl.BlockSpec((S, qw), lambda b, h: (b, h)),
        out_shape=jax.ShapeDtypeStruct((total_rows, N_HEADS * HEAD_DIM), BF16),
        compiler_params=_params(2),
    )(sink, qkv, qkv, qkv)


def _lat_attn_kernel(prev_ref, sink_ref, q_ref, kp_ref, kc_ref, kn_ref, vp_ref, vc_ref, vn_ref,
                     kctx_ref, vctx_ref, o_ref, *, seq_len):
    del prev_ref
    B = ATT_BLOCK
    hk = pl.program_id(1)
    n = pl.program_id(2)
    q = _stack_heads(q_ref)
    kband = jnp.concatenate([kp_ref[...], kc_ref[...], kn_ref[...]], axis=0)
    vband = jnp.concatenate([vp_ref[...], vc_ref[...], vn_ref[...]], axis=0)
    scale = HEAD_DIM ** -0.5
    s_loc = _dot_nt(q, kband) * scale
    q_pos = n * B + (lax.broadcasted_iota(jnp.int32, (GROUP * B, 1), 0) & (B - 1))
    k_pos = (n - 1) * B + lax.broadcasted_iota(jnp.int32, (1, 3 * B), 1)
    valid = (jnp.abs(k_pos - q_pos) <= WINDOW) & (k_pos >= 0) & (k_pos < seq_len)
    s_loc = jnp.where(valid, s_loc, NEG_INF)
    s_ctx = _dot_nt(q, kctx_ref[...]) * scale
    sink = _sink_column(sink_ref, hk, B)
    m = jnp.maximum(jnp.maximum(jnp.max(s_loc, axis=-1, keepdims=True), jnp.max(s_ctx, axis=-1, keepdims=True)), sink)
    e_loc = jnp.exp(s_loc - m)
    e_ctx = jnp.exp(s_ctx - m)
    den = jnp.sum(e_loc, axis=-1, keepdims=True) + jnp.sum(e_ctx, axis=-1, keepdims=True) + jnp.exp(sink - m)
    o = (jnp.dot((e_loc / den).astype(BF16), vband, preferred_element_type=F32)
         + jnp.dot((e_ctx / den).astype(BF16), vctx_ref[...], preferred_element_type=F32))
    for g in range(GROUP):
        o_ref[:, g * HEAD_DIM:(g + 1) * HEAD_DIM] = o[g * B:(g + 1) * B].astype(o_ref.dtype)


def attn_latent(lay, qkr, kctx, vctx, sink, prev):
    B = ATT_BLOCK
    nb = lay.sample_len // B
    P = kctx.shape[2]
    qw = GROUP * HEAD_DIM
    kb0 = N_HEADS
    vb0 = N_HEADS + N_KV_HEADS
    row0 = lay.prompt_rows // B

    def band(col0, shift):
        def index(b, h, n):
            return (b * nb + jnp.clip(n + shift, 0, nb - 1), col0 + h)
        return pl.BlockSpec((B, HEAD_DIM), index)

    return pl.pallas_call(
        functools.partial(_lat_attn_kernel, seq_len=lay.sample_len),
        grid=(lay.n_sample, N_KV_HEADS, nb),
        in_specs=[pl.BlockSpec(memory_space=pl.ANY),
                  pl.BlockSpec(memory_space=pltpu.SMEM),
                  pl.BlockSpec((B, qw), lambda b, h, n: (b * nb + n, h)),
                  band(kb0, -1), band(kb0, 0), band(kb0, 1),
                  band(vb0, -1), band(vb0, 0), band(vb0, 1),
                  pl.BlockSpec((None, None, P, HEAD_DIM), lambda b, h, n: (b, h, 0, 0)),
                  pl.BlockSpec((None, None, P, HEAD_DIM), lambda b, h, n: (b, h, 0, 0))],
        out_specs=pl.BlockSpec((B, qw), lambda b, h, n: (row0 + b * nb + n, h)),
        out_shape=jax.ShapeDtypeStruct(prev.shape, prev.dtype),
        input_output_aliases={0: 0},
        compiler_params=_params(3),
    )(prev, sink, qkr, qkr, qkr, qkr, qkr, qkr, qkr, kctx, vctx)


def rope_tables_full(T):
    rows = T // GRID_W
    row = jnp.repeat(jnp.arange(rows, dtype=F32), GRID_W)
    col = (jnp.arange(rows * GRID_W) % GRID_W).astype(F32)
    n_freq = HEAD_DIM // 4
    inv = ROPE_BASE ** (-jnp.arange(n_freq, dtype=F32) / n_freq)
    ang = jnp.concatenate([row[:, None] * inv, col[:, None] * inv], axis=-1)
    cos, sin = jnp.cos(ang), jnp.sin(ang)
    return jnp.concatenate([cos, cos], axis=-1), jnp.concatenate([-sin, sin], axis=-1)


def attn_layer(lay, x, h, mod3, layer, w_qkv, w_o, sink, cache_k, cache_v, cosf, sinf):
    D = x.shape[1]
    qkv = token_matmul(h, w_qkv, col_blocks=(0,), tn=512, n_out_cols=w_qkv.shape[1])
    qkr = rope_sample(lay, qkv, cosf, sinf)
    kctx = jnp.swapaxes(cache_k, 1, 2).astype(BF16)
    vctx = jnp.swapaxes(cache_v, 1, 2).astype(BF16)
    o = attn_context(lay, qkv, sink, lay.rows)
    o = attn_latent(lay, qkr, kctx, vctx, sink, o)
    x_new = token_matmul(o, w_o, col_blocks=(0,), tn=512, n_out_cols=D, mode="res",
                         residual=(lay, x, mod3, layer, 2))
    nq = N_HEADS * HEAD_DIM
    nk = N_KV_HEADS * HEAD_DIM
    shape = (lay.n_prompt, lay.prompt_len, N_KV_HEADS, HEAD_DIM)
    k_new = qkv[:lay.prompt_rows, nq:nq + nk].reshape(shape)
    v_new = qkv[:lay.prompt_rows, nq + nk:].reshape(shape)
    return x_new, k_new, v_new


DN_QK_DIM = DN_QK_HEADS * DN_HEAD_DIM
DN_V_DIM = DN_V_HEADS * DN_HEAD_DIM
DN_CONV_DIM = 2 * DN_QK_DIM + DN_V_DIM
DN_LANE_CHUNK = 1024
DN_HALO = 8


def _dn_conv_kernel(prev_ref, cur_ref, next_ref, w_ref, o_ref, buf_ref, *, lay):
    i = pl.program_id(0)
    c = pl.program_id(1)
    tile, W = cur_ref.shape
    halo = DN_HALO
    npt = lay.prompt_rows // tile
    tps = lay.sample_len // tile
    ppt = lay.prompt_len // tile
    pos = jnp.where(i < npt, i % ppt, (i - npt) % tps)
    last = jnp.where(i < npt, ppt - 1, tps - 1)
    buf_ref[0:halo, :] = prev_ref[...] * (pos > 0).astype(F32)
    buf_ref[halo:halo + tile, :] = cur_ref[...]
    buf_ref[halo + tile:, :] = next_ref[...] * (pos < last).astype(F32)
    width = w_ref.shape[0]
    pad = width // 2
    acc = jnp.zeros((tile, W), F32)
    for k in range(width):
        start = halo - pad + k
        acc = acc + buf_ref[start:start + tile, :] * w_ref[k:k + 1, :]
    y = acc * jax.nn.sigmoid(acc)
    q_chunks = DN_QK_DIM // W
    is_norm = c < 2 * q_chunks
    post = jnp.where(c < q_chunks, DN_HEAD_DIM ** -0.5, 1.0)
    for hd in range(W // DN_HEAD_DIM):
        cs = slice(hd * DN_HEAD_DIM, (hd + 1) * DN_HEAD_DIM)
        yh = y[:, cs]
        nrm = yh * lax.rsqrt(jnp.sum(yh * yh, axis=-1, keepdims=True) + EPS) * post
        o_ref[:, cs] = jnp.where(is_norm, nrm, yh)


def dn_conv(lay, proj, conv_w):
    T = proj.shape[0]
    tile, W, halo = ROW_TILE, DN_LANE_CHUNK, DN_HALO
    hb = tile // halo
    n_halo_blocks = T // halo
    width = conv_w.shape[0]
    return pl.pallas_call(
        functools.partial(_dn_conv_kernel, lay=lay),
        grid=(T // tile, DN_CONV_DIM // W),
        in_specs=[pl.BlockSpec((halo, W), lambda i, c: (jnp.maximum(i * hb - 1, 0), c)),
                  pl.BlockSpec((tile, W), lambda i, c: (i, c)),
                  pl.BlockSpec((halo, W), lambda i, c: (jnp.minimum((i + 1) * hb, n_halo_blocks - 1), c)),
                  pl.BlockSpec((width, W), lambda i, c: (0, c))],
        out_specs=pl.BlockSpec((tile, W), lambda i, c: (i, c)),
        out_shape=jax.ShapeDtypeStruct((T, DN_CONV_DIM), F32),
        scratch_shapes=[pltpu.VMEM((tile + 2 * halo, W), F32)],
        compiler_params=_params(2),
    )(proj, proj, proj, conv_w)


def _dn_gate_kernel(ab_ref, alog_ref, dtb_ref, o_ref):
    ab = ab_ref[...]
    z = ab + dtb_ref[...]
    softplus = jnp.maximum(z, 0.0) + jnp.log1p(jnp.exp(-jnp.abs(z)))
    g = -jnp.exp(alog_ref[...]) * softplus
    col = lax.broadcasted_iota(jnp.int32, ab.shape, 1)
    o_ref[...] = jnp.where(col < 2 * DN_V_HEADS, g, jax.nn.sigmoid(ab))


def dn_gates(ab, a_log, dt_bias):
    T, W = ab.shape
    pad = W - 2 * DN_V_HEADS
    alog = jnp.pad(a_log.reshape(1, -1), ((0, 0), (0, pad)))
    dtb = jnp.pad(dt_bias.reshape(1, -1), ((0, 0), (0, pad)))
    tile = 1024
    return pl.pallas_call(
        _dn_gate_kernel,
        grid=(T // tile,),
        in_specs=[pl.BlockSpec((tile, W), lambda i: (i, 0)),
                  pl.BlockSpec((1, W), lambda i: (0, 0)),
                  pl.BlockSpec((1, W), lambda i: (0, 0))],
        out_specs=pl.BlockSpec((tile, W), lambda i: (i, 0)),
        out_shape=jax.ShapeDtypeStruct((T, W), F32),
        compiler_params=_params(1),
    )(ab, alog, dtb)


def _dn_masks(reverse):
    C = DN_CHUNK
    r = lax.broadcasted_iota(jnp.int32, (C, C), 0)
    s = lax.broadcasted_iota(jnp.int32, (C, C), 1)
    r2 = lax.broadcasted_iota(jnp.int32, (C, 2 * C), 0)
    s2 = lax.broadcasted_iota(jnp.int32, (C, 2 * C), 1)
    if reverse:
        cum = (s >= r).astype(F32)
        between = ((r2 < s2) & (s2 < C)) | (s2 == C)
        incl, strict = s >= r, s > r
    else:
        cum = (s <= r).astype(F32)
        between = ((r2 > s2) & (s2 < C)) | (s2 == C)
        incl, strict = s <= r, s < r
    eye = (r == s).astype(F32)
    base = strict & ((r >> 3) == (s >> 3))
    merges = []
    for lb in range(3, C.bit_length() - 1):
        merges.append(strict & ((r >> (lb + 1)) == (s >> (lb + 1))) & ((r >> lb) != (s >> lb)))
    return cum, between.astype(F32), incl, strict, eye, base, merges


def _unit_triangular_inverse(A, eye, base, merges):
    C = A.shape[0]
    A16 = A.astype(BF16)
    a = jnp.where(base, A, 0.0)
    a16 = a.astype(BF16)
    P = eye - a
    X = jnp.dot(a16, a16, preferred_element_type=F32)
    PX = jnp.dot(jnp.concatenate([P, X], axis=0).astype(BF16), X.astype(BF16), preferred_element_type=F32)
    P = P + PX[:C]
    T = P + jnp.dot(P.astype(BF16), PX[C:].astype(BF16), preferred_element_type=F32)
    for mask in merges:
        T16 = T.astype(BF16)
        TA = jnp.dot(T16, A16, preferred_element_type=F32)
        TAT = jnp.dot(TA.astype(BF16), T16, preferred_element_type=F32)
        T = T - jnp.where(mask, TAT, 0.0)
    return T


def _dn_unit(q, k, v, G, Q, g_col, beta_col, S, masks, last_row):
    C = DN_CHUNK
    cum, between, incl, strict, eye, base, merges = masks
    Dp = jnp.dot(cum, g_col * between, precision=HIGHEST, preferred_element_type=F32)
    gc = Dp[:, C:C + 1]
    gl = Dp[last_row:last_row + 1, C:C + 1]
    decay = jnp.exp(jnp.where(incl, Dp[:, :C], NEG_INF))
    A = jnp.where(strict, (beta_col * G) * decay, 0.0)
    P = _unit_triangular_inverse(A, eye, base, merges)
    eg = jnp.exp(gc)
    rhs = jnp.concatenate([v * beta_col, k * (beta_col * eg)], axis=1).astype(BF16)
    UW = jnp.dot(P.astype(BF16), rhs, preferred_element_type=F32)
    DV = v.shape[1]
    u, w = UW[:, :DV], UW[:, DV:]
    S16 = S.astype(BF16)
    WQ = jnp.dot(jnp.concatenate([w, q * eg], axis=0).astype(BF16), S16, preferred_element_type=F32)
    v_new = u - WQ[:C]
    vn16 = v_new.astype(BF16)
    o = WQ[C:] + jnp.dot((Q * decay).astype(BF16), vn16, preferred_element_type=F32)
    kg = (k * jnp.exp(gl - gc)).astype(BF16)
    S_new = S * jnp.exp(gl) + lax.dot_general(kg, vn16, (((0,), (0,)), ((), ())), preferred_element_type=F32)
    return o, S_new


def _dn_chunk_kernel(*refs, use_s0, aliased):
    it = iter(refs)
    if aliased:
        next(it), next(it)
    fwd = [next(it) for _ in range(4)]
    bwd = [next(it) for _ in range(4)]
    if use_s0:
        s0f_ref, s0b_ref = next(it), next(it)
    of_ref, ob_ref, sf_ref, sb_ref, S_ref = [next(it) for _ in range(5)]
    t = pl.program_id(2)
    n_t = pl.num_programs(2)
    rep = DN_V_HEADS // DN_QK_HEADS
    C, HD = DN_CHUNK, DN_HEAD_DIM

    @pl.when(t == 0)
    def _():
        if use_s0:
            S_ref[0:rep] = s0f_ref[...]
            S_ref[rep:2 * rep] = s0b_ref[...]
        else:
            S_ref[...] = jnp.zeros(S_ref.shape, F32)

    tile = fwd[0].shape[0]
    n_chunks = tile // C
    for d, (q_ref, k_ref, v_ref, gb_ref) in enumerate((fwd, bwd)):
        reverse = d == 1
        masks = _dn_masks(reverse)
        o_ref = ob_ref if reverse else of_ref
        order = range(n_chunks - 1, -1, -1) if reverse else range(n_chunks)
        for ci in order:
            rows = slice(ci * C, (ci + 1) * C)
            q = q_ref[rows, :]
            k = k_ref[rows, :]
            k16 = k.astype(BF16)
            G = _dot_nt(k16, k16)
            Q = _dot_nt(q.astype(BF16), k16)
            for r in range(rep):
                g_col = gb_ref[rows, d * rep + r:d * rep + r + 1]
                beta_col = gb_ref[rows, 2 * rep + d * rep + r:2 * rep + d * rep + r + 1]
                v = v_ref[rows, r * HD:(r + 1) * HD]
                o, S_new = _dn_unit(q, k, v, G, Q, g_col, beta_col, S_ref[d * rep + r], masks,
                                    0 if reverse else C - 1)
                S_ref[d * rep + r] = S_new
                o_ref[rows, r * HD:(r + 1) * HD] = o

    @pl.when(t == n_t - 1)
    def _():
        sf_ref[...] = S_ref[0:rep]
        sb_ref[...] = S_ref[rep:2 * rep]


def dn_chunks(qkvc, gbh, n_seq, seq_len, row0, s0=None, prev=None):
    T = qkvc.shape[0]
    tile = ROW_TILE
    tiles = seq_len // tile
    t0 = row0 // tile
    rep = DN_V_HEADS // DN_QK_HEADS
    HD = DN_HEAD_DIM
    kb0 = DN_QK_HEADS
    vb0 = 2 * DN_QK_DIM // (rep * HD)

    def row_f(b, t):
        return t0 + b * tiles + t

    def row_b(b, t):
        return t0 + b * tiles + (tiles - 1 - t)

    def specs(row):
        return [pl.BlockSpec((tile, HD), lambda b, h, t: (row(b, t), h)),
                pl.BlockSpec((tile, HD), lambda b, h, t: (row(b, t), kb0 + h)),
                pl.BlockSpec((tile, rep * HD), lambda b, h, t: (row(b, t), vb0 + h)),
                pl.BlockSpec((None, tile, 4 * rep), lambda b, h, t: (h, row(b, t), 0))]

    in_specs = specs(row_f) + specs(row_b)
    args = [qkvc, qkvc, qkvc, gbh] * 2
    state_spec = pl.BlockSpec((None, rep, HD, HD), lambda b, h, t: (b, h, 0, 0))
    if s0 is not None:
        in_specs += [state_spec, state_spec]
        args += list(s0)
    aliases = {}
    if prev is not None:
        in_specs = [pl.BlockSpec(memory_space=pl.ANY)] * 2 + in_specs
        args = list(prev) + args
        aliases = {0: 0, 1: 1}
    o_shape = jax.ShapeDtypeStruct((T, DN_V_DIM), F32)
    s_shape = jax.ShapeDtypeStruct((n_seq, DN_V_HEADS, HD, HD), F32)
    return pl.pallas_call(
        functools.partial(_dn_chunk_kernel, use_s0=s0 is not None, aliased=prev is not None),
        grid=(n_seq, DN_QK_HEADS, tiles),
        in_specs=in_specs,
        out_specs=[pl.BlockSpec((tile, rep * HD), lambda b, h, t: (row_f(b, t), h)),
                   pl.BlockSpec((tile, rep * HD), lambda b, h, t: (row_b(b, t), h)),
                   state_spec, state_spec],
        out_shape=[o_shape, o_shape, s_shape, s_shape],
        scratch_shapes=[pltpu.VMEM((2 * rep, HD, HD), F32)],
        input_output_aliases=aliases,
        compiler_params=_params(3),
    )(*args)


def _dn_out_kernel(of_ref, ob_ref, z_ref, g_ref, y_ref):
    g = g_ref[...]
    for hd in range(of_ref.shape[1] // DN_HEAD_DIM):
        cs = slice(hd * DN_HEAD_DIM, (hd + 1) * DN_HEAD_DIM)
        o = of_ref[:, cs] + ob_ref[:, cs]
        z = z_ref[:, cs]
        y = o * lax.rsqrt(jnp.mean(o * o, axis=-1, keepdims=True) + EPS) * g
        y_ref[:, cs] = (y * (z * jax.nn.sigmoid(z))).astype(y_ref.dtype)


def dn_gated_norm(o_f, o_b, proj, norm_g):
    T = o_f.shape[0]
    tile, W = ROW_TILE, DN_LANE_CHUNK
    zb0 = DN_CONV_DIM // W
    return pl.pallas_call(
        _dn_out_kernel,
        grid=(T // tile, DN_V_DIM // W),
        in_specs=[pl.BlockSpec((tile, W), lambda i, c: (i, c)),
                  pl.BlockSpec((tile, W), lambda i, c: (i, c)),
                  pl.BlockSpec((tile, W), lambda i, c: (i, zb0 + c)),
                  pl.BlockSpec((1, DN_HEAD_DIM), lambda i, c: (0, 0))],
        out_specs=pl.BlockSpec((tile, W), lambda i, c: (i, c)),
        out_shape=jax.ShapeDtypeStruct((T, DN_V_DIM), BF16),
        compiler_params=_params(2),
    )(o_f, o_b, proj, norm_g.reshape(1, -1))


def dn_layer(lay, x, h, mod3, layer, w_in, conv_w, a_log, dt_bias, norm_g, w_out, s_fwd, s_bwd):
    T, D = x.shape
    main = DN_CONV_DIM + DN_V_DIM
    n_gate = w_in.shape[1] - main
    proj = token_matmul(h, w_in, col_blocks=(0,), tn=512, n_out_cols=main)
    ab = token_matmul(h, w_in, col_blocks=(main // n_gate,), tn=n_gate, n_out_cols=n_gate)
    qkvc = dn_conv(lay, proj, conv_w)
    gb = dn_gates(ab, a_log, dt_bias)
    rep = DN_V_HEADS // DN_QK_HEADS
    gbh = gb.reshape(T, 4, DN_QK_HEADS, rep).transpose(2, 0, 1, 3).reshape(DN_QK_HEADS, T, 4 * rep)
    o_f, o_b, sf, sb = dn_chunks(qkvc, gbh, lay.n_prompt, lay.prompt_len, 0)
    o_f, o_b, _, _ = dn_chunks(qkvc, gbh, lay.n_sample, lay.sample_len, lay.prompt_rows,
                               s0=(s_fwd, s_bwd), prev=(o_f, o_b))
    y = dn_gated_norm(o_f, o_b, proj, norm_g)
    x_new = token_matmul(y, w_out, col_blocks=(0,), tn=512, n_out_cols=D, mode="res",
                         residual=(lay, x, mod3, layer, 2))
    return x_new, sf, sb


def _rmsnorm_kernel(x_ref, g_ref, o_ref):
    x = x_ref[...]
    o_ref[...] = x * lax.rsqrt(jnp.mean(x * x, axis=-1, keepdims=True) + EPS) * g_ref[...]


def final_norm(x, g, row0, n_rows):
    D = x.shape[1]
    tile = ROW_TILE
    t0 = row0 // tile
    return pl.pallas_call(
        _rmsnorm_kernel,
        grid=(n_rows // tile,),
        in_specs=[pl.BlockSpec((tile, D), lambda i: (t0 + i, 0)),
                  pl.BlockSpec((1, D), lambda i: (0, 0))],
        out_specs=pl.BlockSpec((tile, D), lambda i: (i, 0)),
        out_shape=jax.ShapeDtypeStruct((n_rows, D), F32),
        compiler_params=_params(1),
    )(x, g.reshape(1, D))


def kernel(x_prompt, x_sample, cache_attn_k, cache_attn_v, state_dn_fwd, state_dn_bwd, c, c_ctx, w_ada, b_ada, norm1_g, norm2_g, final_g, conv_w_pw1, conv_b_pw1, conv_w_dw, conv_b_dw, conv_norm_g, conv_w_pw2, conv_b_pw2, attn_w_qkv, attn_w_o, attn_sink, dn_w_in, dn_conv_w, dn_A_log, dn_dt_bias, dn_norm_g, dn_w_out, moe_w_router, moe_w_gate, moe_w_up, moe_w_down):
    n_prompt, prompt_len, D = x_prompt.shape
    n_sample, sample_len, _ = x_sample.shape
    depth = w_ada.shape[0]
    assert n_sample + 1 <= MOD_ROWS
    assert prompt_len % ROW_TILE == 0 and sample_len % ROW_TILE == 0
    lay = Layout(n_prompt, prompt_len, n_sample, sample_len)
    x = jnp.concatenate([x_prompt.reshape(-1, D), x_sample.reshape(-1, D)], axis=0)
    cvec = jnp.concatenate([c_ctx[None], c, jnp.zeros((MOD_ROWS - 1 - n_sample, D), F32)], axis=0)
    mod = adaln_all(cvec, w_ada, b_ada)
    mod3 = mod.reshape(depth * MOD_ROWS * N_MOD, 1, D)
    cosf, sinf = rope_tables_full(sample_len)
    new_k, new_v, new_sf, new_sb = [], [], [], []
    for l in range(depth):
        j, kind = divmod(l, 3)
        (h,) = modulate(lay, x, norm1_g[l], mod3, l, 0, 1)
        if kind == 0:
            x = conv_layer(lay, x, h, mod3, l, conv_w_pw1[j], conv_b_pw1[j], conv_w_dw[j], conv_b_dw[j],
                           conv_norm_g[j], conv_w_pw2[j], conv_b_pw2[j])
        elif kind == 1:
            x, k_ctx, v_ctx = attn_layer(lay, x, h, mod3, l, attn_w_qkv[j], attn_w_o[j], attn_sink[j],
                                         cache_attn_k[:, j], cache_attn_v[:, j], cosf, sinf)
            new_k.append(k_ctx)
            new_v.append(v_ctx)
        else:
            x, s_f, s_b = dn_layer(lay, x, h, mod3, l, dn_w_in[j], dn_conv_w[j], dn_A_log[j], dn_dt_bias[j],
                                   dn_norm_g[j], dn_w_out[j], state_dn_fwd[:, j], state_dn_bwd[:, j])
            new_sf.append(s_f)
            new_sb.append(s_b)
        h, aff = modulate(lay, x, norm2_g[l], mod3, l, 3, 4, moe_w_router[l])
        x = moe_layer(lay, x, h, aff, mod3, l, moe_w_gate[l], moe_w_up[l], moe_w_down[l])
    y_prompt = final_norm(x, final_g, 0, lay.prompt_rows).reshape(x_prompt.shape)
    y_sample = final_norm(x, final_g, lay.prompt_rows, lay.sample_rows).reshape(x_sample.shape)
    return (y_prompt, y_sample, jnp.stack(new_k, axis=1), jnp.stack(new_v, axis=1),
            jnp.stack(new_sf, axis=1), jnp.stack(new_sb, axis=1))
```
